```python
import math
import jax, jax.numpy as jnp
from jax import lax
import numpy as np

D_MODEL = 1024
BATCH = 8
SEQ = 4096
DEPTH = 1
DEC_BATCH = 32
DEC_SEQ = 2048
PAST_LEN = 128

N_HEADS = 4
HEAD_DIM = 64
V_DIM = 2 * HEAD_DIM
ATT_WIDTH = N_HEADS * V_DIM
ROT_DIM = HEAD_DIM // 4
ROPE_THETA = 500000.0
Q_BLOCK = 128
CONV_CH = 512
CONV_K = 31
D_FF = 2816
N_BRANCH = 2
LN_EPS = 1e-5
ALPHA = (2.0 * DEPTH) ** 0.25
BETA = (8.0 * DEPTH) ** -0.25
Q_COLS = N_HEADS * 2 * HEAD_DIM
IN_COLS = 2 * Q_COLS + ATT_WIDTH + 2 * CONV_CH + N_BRANCH * D_MODEL

kernel_name = "hybrid_diffattn_conformer_gated_encoder"


def layer_norm(x, g, b):
    xf = x.astype(jnp.float32)
    mu = jnp.mean(xf, axis=-1, keepdims=True)
    var = jnp.mean(jnp.square(xf - mu), axis=-1, keepdims=True)
    y = (xf - mu) * lax.rsqrt(var + LN_EPS) * g.astype(jnp.float32) + b.astype(jnp.float32)
    return y.astype(x.dtype)


def rms_norm(x, g):
    xf = x.astype(jnp.float32)
    ms = jnp.mean(jnp.square(xf), axis=-1, keepdims=True)
    return (xf * lax.rsqrt(ms + LN_EPS) * g.astype(jnp.float32)).astype(x.dtype)


def swiglu(x, w_gate, w_up, w_down):
    return (jax.nn.silu(x @ w_gate) * (x @ w_up)) @ w_down


def rope_tables(seq, dtype):
    inv = ROPE_THETA ** (-jnp.arange(0, ROT_DIM, 2, dtype=jnp.float32) / ROT_DIM)
    ang = jnp.arange(seq, dtype=jnp.float32)[:, None] * inv[None, :]
    cos = jnp.cos(ang)[:, None, None, :].astype(dtype)
    sin = jnp.sin(ang)[:, None, None, :].astype(dtype)
    return cos, sin


def apply_rope(x, cos, sin):
    half = ROT_DIM // 2
    x1 = x[..., :half]
    x2 = x[..., half:ROT_DIM]
    xp = x[..., ROT_DIM:]
    return jnp.concatenate([x1 * cos - x2 * sin, x2 * cos + x1 * sin, xp], axis=-1)


def diff_attention(q, k, v, lam):
    B, S = q.shape[0], q.shape[1]
    nb = S // Q_BLOCK
    qb = (q * (HEAD_DIM ** -0.5)).reshape(B, nb, Q_BLOCK, N_HEADS, 2, HEAD_DIM)
    qb = qb.transpose(1, 0, 2, 3, 4, 5)

    def block(q_blk):
        s = jnp.einsum('bqhmd,bkhmd->bmhqk', q_blk, k)
        p = jax.nn.softmax(s.astype(jnp.float32), axis=-1)
        w = p[:, 0] - lam * p[:, 1]
        return jnp.einsum('bhqk,bkhe->bqhe', w.astype(v.dtype), v)

    o = lax.map(block, qb)
    return o.transpose(1, 0, 2, 3, 4).reshape(B, S, N_HEADS, V_DIM)


def conv_module(u, dw_w, dw_b, ln_g, ln_b, w_proj):
    y = lax.conv_general_dilated(
        u, dw_w.astype(u.dtype), window_strides=(1,),
        padding=[(CONV_K // 2, CONV_K // 2)],
        dimension_numbers=('NWC', 'WIO', 'NWC'),
        feature_group_count=CONV_CH) + dw_b
    y = jax.nn.silu(layer_norm(y, ln_g, ln_b))
    return y @ w_proj


def encoder_layer(x, l, p):
    lambda_init = 0.8 - 0.6 * math.exp(-0.3 * l)
    B, S, _ = x.shape
    h = swiglu(x, p['ffn1_w_gate'][l], p['ffn1_w_up'][l], p['ffn1_w_down'][l])
    x = layer_norm(ALPHA * x + 0.5 * h, p['ln1_g'][l], p['ln1_b'][l])
    z = x @ p['w_in'][l]
    c1 = Q_COLS
    c2 = 2 * Q_COLS
    c3 = c2 + ATT_WIDTH
    c4 = c3 + 2 * CONV_CH
    q, k, v, z_conv, z_gate = jnp.split(z, [c1, c2, c3, c4], axis=-1)
    cos, sin = rope_tables(S, x.dtype)
    q = apply_rope(q.reshape(B, S, N_HEADS, 2, HEAD_DIM), cos, sin)
    k = apply_rope(k.reshape(B, S, N_HEADS, 2, HEAD_DIM), cos, sin)
    v = v.reshape(B, S, N_HEADS, V_DIM)
    f32 = jnp.float32
    lam = (jnp.exp(jnp.sum(p['lambda_q1'][l].astype(f32) * p['lambda_k1'][l].astype(f32)))
           - jnp.exp(jnp.sum(p['lambda_q2'][l].astype(f32) * p['lambda_k2'][l].astype(f32)))
           + lambda_init)
    o = diff_attention(q, k, v, lam)
    o = rms_norm(o, p['subln_g'][l]) * (1.0 - lambda_init)
    a = o.reshape(B, S, ATT_WIDTH) @ p['w_att_proj'][l]
    za, zb = jnp.split(z_conv, 2, axis=-1)
    u = za * jax.nn.sigmoid(zb)
    c = conv_module(u, p['conv_dw_w'][l], p['conv_dw_b'][l],
                    p['conv_ln_g'][l], p['conv_ln_b'][l], p['w_conv_proj'][l])
    g = jax.nn.sigmoid(z_gate.reshape(B, S, N_BRANCH, D_MODEL) + p['b_gate'][l])
    m = g[:, :, 0] * a + g[:, :, 1] * c
    x = layer_norm(ALPHA * x + m @ p['w_out'][l], p['ln2_g'][l], p['ln2_b'][l])
    h = swiglu(x, p['ffn2_w_gate'][l], p['ffn2_w_up'][l], p['ffn2_w_down'][l])
    x = layer_norm(ALPHA * x + 0.5 * h, p['ln3_g'][l], p['ln3_b'][l])
    return x


def encoder_stack(x, p):
    for l in range(DEPTH):
        x = encoder_layer(x, l, p)
    return x


def setup_inputs(seed: int = 0) -> dict:
    key = jax.random.key(seed)
    ks = jax.random.split(key, 32)
    f32 = jnp.float32

    def nrm(k, shape, scale):
        return jax.random.normal(k, shape, f32) * scale

    L = DEPTH
    d = {}
    d['x_prompt'] = nrm(ks[0], (BATCH, SEQ, D_MODEL), 1.0)
    d['x_sample'] = nrm(ks[1], (DEC_BATCH, DEC_SEQ, D_MODEL), 1.0)
    d['ffn1_w_gate'] = nrm(ks[2], (L, D_MODEL, D_FF), D_MODEL ** -0.5)
    d['ffn1_w_up'] = nrm(ks[3], (L, D_MODEL, D_FF), D_MODEL ** -0.5)
    d['ffn1_w_down'] = nrm(ks[4], (L, D_FF, D_MODEL), BETA * D_FF ** -0.5)
    d['ln1_g'] = 1.0 + nrm(ks[5], (L, D_MODEL), 0.02)
    d['ln1_b'] = nrm(ks[6], (L, D_MODEL), 0.02)
    d['w_in'] = nrm(ks[7], (L, D_MODEL, IN_COLS), D_MODEL ** -0.5)
    d['b_gate'] = nrm(ks[8], (L, N_BRANCH, D_MODEL), 0.1)
    d['lambda_q1'] = nrm(ks[9], (L, HEAD_DIM), 0.1)
    d['lambda_k1'] = nrm(ks[10], (L, HEAD_DIM), 0.1)
    d['lambda_q2'] = nrm(ks[11], (L, HEAD_DIM), 0.1)
    d['lambda_k2'] = nrm(ks[12], (L, HEAD_DIM), 0.1)
    d['subln_g'] = 1.0 + nrm(ks[13], (L, V_DIM), 0.02)
    d['w_att_proj'] = nrm(ks[14], (L, ATT_WIDTH, D_MODEL), BETA * ATT_WIDTH ** -0.5)
    d['conv_dw_w'] = nrm(ks[15], (L, CONV_K, 1, CONV_CH), CONV_K ** -0.5)
    d['conv_dw_b'] = nrm(ks[16], (L, CONV_CH), 0.02)
    d['conv_ln_g'] = 1.0 + nrm(ks[17], (L, CONV_CH), 0.02)
    d['conv_ln_b'] = nrm(ks[18], (L, CONV_CH), 0.02)
    d['w_conv_proj'] = nrm(ks[19], (L, CONV_CH, D_MODEL), BETA * CONV_CH ** -0.5)
    d['w_out'] = nrm(ks[20], (L, D_MODEL, D_MODEL), BETA * D_MODEL ** -0.5)
    d['ln2_g'] = 1.0 + nrm(ks[21], (L, D_MODEL), 0.02)
    d['ln2_b'] = nrm(ks[22], (L, D_MODEL), 0.02)
    d['ffn2_w_gate'] = nrm(ks[23], (L, D_MODEL, D_FF), D_MODEL ** -0.5)
    d['ffn2_w_up'] = nrm(ks[24], (L, D_MODEL, D_FF), D_MODEL ** -0.5)
    d['ffn2_w_down'] = nrm(ks[25], (L, D_FF, D_MODEL), BETA * D_FF ** -0.5)
    d['ln3_g'] = 1.0 + nrm(ks[26], (L, D_MODEL), 0.02)
    d['ln3_b'] = nrm(ks[27], (L, D_MODEL), 0.02)
    return d


def reference(x_prompt, x_sample, ffn1_w_gate, ffn1_w_up, ffn1_w_down, ln1_g, ln1_b,
              w_in, b_gate, lambda_q1, lambda_k1, lambda_q2, lambda_k2, subln_g,
              w_att_proj, conv_dw_w, conv_dw_b, conv_ln_g, conv_ln_b, w_conv_proj,
              w_out, ln2_g, ln2_b, ffn2_w_gate, ffn2_w_up, ffn2_w_down, ln3_g, ln3_b):
    p = dict(ffn1_w_gate=ffn1_w_gate, ffn1_w_up=ffn1_w_up, ffn1_w_down=ffn1_w_down,
             ln1_g=ln1_g, ln1_b=ln1_b, w_in=w_in, b_gate=b_gate,
             lambda_q1=lambda_q1, lambda_k1=lambda_k1, lambda_q2=lambda_q2, lambda_k2=lambda_k2,
             subln_g=subln_g, w_att_proj=w_att_proj, conv_dw_w=conv_dw_w, conv_dw_b=conv_dw_b,
             conv_ln_g=conv_ln_g, conv_ln_b=conv_ln_b, w_conv_proj=w_conv_proj, w_out=w_out,
             ln2_g=ln2_g, ln2_b=ln2_b, ffn2_w_gate=ffn2_w_gate, ffn2_w_up=ffn2_w_up,
             ffn2_w_down=ffn2_w_down, ln3_g=ln3_g, ln3_b=ln3_b)
    y_prompt = encoder_stack(x_prompt, p)
    y_sample = encoder_stack(x_sample, p)
    return (y_prompt, y_sample)
```

```python
import functools
import math

import jax
import jax.numpy as jnp
from jax import lax
from jax.experimental import pallas as pl
from jax.experimental.pallas import tpu as pltpu

D_MODEL = 1024
DEPTH = 1
N_HEADS = 4
HEAD_DIM = 64
V_DIM = 2 * HEAD_DIM
ATT_WIDTH = N_HEADS * V_DIM
ROT_DIM = HEAD_DIM // 4
ROPE_THETA = 500000.0
CONV_CH = 512
CONV_K = 31
D_FF = 2816
N_BRANCH = 2
LN_EPS = 1e-5
ALPHA = (2.0 * DEPTH) ** 0.25
Q_COLS = N_HEADS * 2 * HEAD_DIM
IN_COLS = 2 * Q_COLS + ATT_WIDTH + 2 * CONV_CH + N_BRANCH * D_MODEL

LANES = 128
BF16_SUBLANES = 16
VMEM_LIMIT_BYTES = 56 * 1024 * 1024

FFN_ROWS = 512
PROJ_ROWS = 512
ATTN_Q_ROWS = 128
MIX_ROWS = 256
CONV_HALO = BF16_SUBLANES
CONV_CHUNK = 32

f32 = jnp.float32
bf16 = jnp.bfloat16


def _resident(shape):
    return pl.BlockSpec(shape, lambda *_: (0,) * len(shape), pipeline_mode=pl.Buffered(1))


def _layer_norm(x, g, b):
    mu = jnp.mean(x, axis=-1, keepdims=True)
    xc = x - mu
    var = jnp.mean(xc * xc, axis=-1, keepdims=True)
    return xc * lax.rsqrt(var + LN_EPS) * g + b


def _silu(x):
    return x * jax.nn.sigmoid(x)


def _dot(a, b):
    return jnp.dot(a, b, preferred_element_type=f32)


def _ffn_ln_kernel(x_ref, wg_ref, wu_ref, wd_ref, g_ref, b_ref, o_ref):
    x = x_ref[...]
    xb = x.astype(bf16)
    gate = _dot(xb, wg_ref[...])
    up = _dot(xb, wu_ref[...])
    h = (_silu(gate) * up).astype(bf16)
    y = _dot(h, wd_ref[...])
    o_ref[...] = _layer_norm(ALPHA * x + 0.5 * y, g_ref[...], b_ref[...])


def _ffn_ln(x, wg, wu, wd, g, b):
    n = x.shape[0]
    return pl.pallas_call(
        _ffn_ln_kernel,
        grid=(n // FFN_ROWS,),
        in_specs=[
            pl.BlockSpec((FFN_ROWS, D_MODEL), lambda i: (i, 0)),
            _resident((D_MODEL, D_FF)),
            _resident((D_MODEL, D_FF)),
            _resident((D_FF, D_MODEL)),
            _resident((1, D_MODEL)),
            _resident((1, D_MODEL)),
        ],
        out_specs=pl.BlockSpec((FFN_ROWS, D_MODEL), lambda i: (i, 0)),
        out_shape=jax.ShapeDtypeStruct((n, D_MODEL), f32),
        compiler_params=pltpu.CompilerParams(
            dimension_semantics=("arbitrary",), vmem_limit_bytes=VMEM_LIMIT_BYTES),
        name="ffn_ln",
    )(x, wg, wu, wd, g, b)


def _rope(z, cos_t, sin_lo, sin_hi):
    half = ROT_DIM // 2
    right = pltpu.roll(z, LANES - half, axis=1)
    left = pltpu.roll(z, half, axis=1)
    return z * cos_t + right * sin_lo + left * sin_hi


def _in_proj_kernel(x_ref, w_ref, bg_ref, cos_ref, slo_ref, shi_ref,
                    q_ref, k_ref, v_ref, u_ref, gate_ref):
    xb = x_ref[0].astype(bf16)
    cos_t, sin_lo, sin_hi = cos_ref[...], slo_ref[...], shi_ref[...]
    c_k, c_v, c_conv, c_gate = Q_COLS, 2 * Q_COLS, 2 * Q_COLS + ATT_WIDTH, 2 * Q_COLS + ATT_WIDTH + 2 * CONV_CH
    for s in range(Q_COLS // LANES):
        lo = s * LANES
        zq = _dot(xb, w_ref[:, lo:lo + LANES])
        q_ref[0, :, lo:lo + LANES] = (_rope(zq, cos_t, sin_lo, sin_hi) * (HEAD_DIM ** -0.5)).astype(bf16)
        zk = _dot(xb, w_ref[:, c_k + lo:c_k + lo + LANES])
        k_ref[0, :, lo:lo + LANES] = _rope(zk, cos_t, sin_lo, sin_hi).astype(bf16)
    v_ref[0] = _dot(xb, w_ref[:, c_v:c_conv]).astype(bf16)
    za = _dot(xb, w_ref[:, c_conv:c_conv + CONV_CH])
    zb = _dot(xb, w_ref[:, c_conv + CONV_CH:c_gate])
    u_ref[0] = (za * jax.nn.sigmoid(zb)).astype(bf16)
    zg = _dot(xb, w_ref[:, c_gate:])
    gate_ref[0] = jax.nn.sigmoid(zg + bg_ref[...]).astype(bf16)


def _in_proj(x1, w_in, b_gate, cos_t, sin_lo, sin_hi):
    bsz, seq, _ = x1.shape
    rows = PROJ_ROWS
    tok = lambda width: pl.BlockSpec((1, rows, width), lambda b, i: (b, i, 0))
    tab = pl.BlockSpec((rows, LANES), lambda b, i: (i, 0))
    out = lambda width: jax.ShapeDtypeStruct((bsz, seq, width), bf16)
    return pl.pallas_call(
        _in_proj_kernel,
        grid=(bsz, seq // rows),
        in_specs=[tok(D_MODEL), _resident((D_MODEL, IN_COLS)), _resident((1, N_BRANCH * D_MODEL)),
                  tab, tab, tab],
        out_specs=[tok(Q_COLS), tok(Q_COLS), tok(ATT_WIDTH), tok(CONV_CH), tok(N_BRANCH * D_MODEL)],
        out_shape=[out(Q_COLS), out(Q_COLS), out(ATT_WIDTH), out(CONV_CH), out(N_BRANCH * D_MODEL)],
        compiler_params=pltpu.CompilerParams(
            dimension_semantics=("arbitrary", "arbitrary"), vmem_limit_bytes=VMEM_LIMIT_BYTES),
        name="in_proj",
    )(x1, w_in, b_gate, cos_t, sin_lo, sin_hi)


def _rope_tables(seq):
    half = ROT_DIM // 2
    inv = ROPE_THETA ** (-jnp.arange(0, ROT_DIM, 2, dtype=f32) / ROT_DIM)
    ang = jnp.arange(seq, dtype=f32)[:, None] * inv[None, :]
    cos, sin = jnp.cos(ang), jnp.sin(ang)
    zeros = jnp.zeros((seq, HEAD_DIM - ROT_DIM), f32)
    zh = jnp.zeros((seq, half), f32)
    cos_t = jnp.concatenate([cos, cos, zeros + 1.0], axis=1)
    sin_lo = jnp.concatenate([-sin, zh, zeros], axis=1)
    sin_hi = jnp.concatenate([zh, sin, zeros], axis=1)
    rep = LANES // HEAD_DIM
    return tuple(jnp.tile(t, (1, rep)) for t in (cos_t, sin_lo, sin_hi))


def _diff_attn_kernel(lambda_init, q_ref, k_ref, v_ref, lq1_ref, lk1_ref, lq2_ref, lk2_ref, sg_ref, o_ref):
    q = q_ref[0]
    k = k_ref[0]
    v = v_ref[0]
    tq = q.shape[0]
    lane = lax.broadcasted_iota(jnp.int32, q.shape, 1)
    zero = jnp.zeros_like(q)
    q2 = jnp.concatenate([jnp.where(lane < HEAD_DIM, q, zero), jnp.where(lane >= HEAD_DIM, q, zero)], axis=0)
    s = lax.dot_general(q2, k, (((1,), (1,)), ((), ())), preferred_element_type=f32)
    m = jnp.max(s, axis=-1, keepdims=True)
    e = jnp.exp(s - m)
    l = jnp.sum(e, axis=-1, keepdims=True)
    o = _dot(e.astype(bf16), v) / l
    lam = (jnp.exp(jnp.sum(lq1_ref[...] * lk1_ref[...], axis=-1, keepdims=True))
           - jnp.exp(jnp.sum(lq2_ref[...] * lk2_ref[...], axis=-1, keepdims=True)) + lambda_init)
    d = o[:tq] - lam * o[tq:]
    ms = jnp.mean(d * d, axis=-1, keepdims=True)
    o_ref[0] = (d * lax.rsqrt(ms + LN_EPS) * sg_ref[...] * (1.0 - lambda_init)).astype(bf16)


def _diff_attn(q, k, v, lq1, lk1, lq2, lk2, subln_g, lambda_init):
    bsz, seq, _ = q.shape
    tq = ATTN_Q_ROWS
    kv = pl.BlockSpec((1, seq, V_DIM), lambda b, h, i: (b, 0, h))
    qo = pl.BlockSpec((1, tq, V_DIM), lambda b, h, i: (b, i, h))
    return pl.pallas_call(
        functools.partial(_diff_attn_kernel, lambda_init),
        grid=(bsz, N_HEADS, seq // tq),
        in_specs=[qo, kv, kv] + [_resident((1, HEAD_DIM))] * 4 + [_resident((1, V_DIM))],
        out_specs=qo,
        out_shape=jax.ShapeDtypeStruct((bsz, seq, ATT_WIDTH), bf16),
        compiler_params=pltpu.CompilerParams(
            dimension_semantics=("arbitrary",) * 3, vmem_limit_bytes=VMEM_LIMIT_BYTES),
        name="diff_attn",
    )(q, k, v, lq1, lk1, lq2, lk2, subln_g)


def _mix_kernel(o_ref, u_ref, up_ref, un_ref, gate_ref, x_ref,
                dw_ref, db_ref, cg_ref, cb_ref, wc_ref, wa_ref, wo_ref, g2_ref, b2_ref,
                y_ref, ubuf_ref):
    i = pl.program_id(1)
    rows = u_ref.shape[1]
    prev = up_ref[0].astype(f32)
    nxt = un_ref[0].astype(f32)
    ubuf_ref[0:CONV_HALO, :] = jnp.where(i > 0, prev, jnp.zeros_like(prev))
    ubuf_ref[CONV_HALO:CONV_HALO + rows, :] = u_ref[0].astype(f32)
    ubuf_ref[CONV_HALO + rows:, :] = jnp.where(i < pl.num_programs(1) - 1, nxt, jnp.zeros_like(nxt))

    first = CONV_HALO - CONV_K // 2
    chunks = []
    for r in range(0, rows, CONV_CHUNK):
        acc = jnp.zeros((CONV_CHUNK, CONV_CH), f32)
        for j in range(CONV_K):
            acc = acc + dw_ref[j:j + 1, :] * ubuf_ref[r + first + j:r + first + j + CONV_CHUNK, :]
        chunks.append(acc)
    conv = jnp.concatenate(chunks, axis=0) + db_ref[...]
    c_in = _silu(_layer_norm(conv, cg_ref[...], cb_ref[...])).astype(bf16)
    c = _dot(c_in, wc_ref[...])
    a = _dot(o_ref[0], wa_ref[...])
    gate = gate_ref[0].astype(f32)
    m = (gate[:, :D_MODEL] * a + gate[:, D_MODEL:] * c).astype(bf16)
    y_ref[0] = _layer_norm(ALPHA * x_ref[0] + _dot(m, wo_ref[...]), g2_ref[...], b2_ref[...])


def _mix(o, u, gates, x1, dw_w, dw_b, cg, cb, wc, wa, wo, g2, b2):
    bsz, seq, _ = x1.shape
    rows = MIX_ROWS
    per_tile = rows // CONV_HALO
    last_halo = seq // CONV_HALO - 1
    tok = lambda width: pl.BlockSpec((1, rows, width), lambda b, i: (b, i, 0))
    halo_prev = pl.BlockSpec((1, CONV_HALO, CONV_CH), lambda b, i: (b, jnp.maximum(i * per_tile - 1, 0), 0))
    halo_next = pl.BlockSpec((1, CONV_HALO, CONV_CH), lambda b, i: (b, jnp.minimum((i + 1) * per_tile, last_halo), 0))
    return pl.pallas_call(
        _mix_kernel,
        grid=(bsz, seq // rows),
        in_specs=[tok(ATT_WIDTH), tok(CONV_CH), halo_prev, halo_next, tok(N_BRANCH * D_MODEL), tok(D_MODEL),
                  _resident((CONV_K, CONV_CH)), _resident((1, CONV_CH)), _resident((1, CONV_CH)),
                  _resident((1, CONV_CH)), _resident((CONV_CH, D_MODEL)), _resident((ATT_WIDTH, D_MODEL)),
                  _resident((D_MODEL, D_MODEL)), _resident((1, D_MODEL)), _resident((1, D_MODEL))],
        out_specs=tok(D_MODEL),
        out_shape=jax.ShapeDtypeStruct((bsz, seq, D_MODEL), f32),
        scratch_shapes=[pltpu.VMEM((rows + 2 * CONV_HALO, CONV_CH), f32)],
        compiler_params=pltpu.CompilerParams(
            dimension_semantics=("arbitrary", "arbitrary"), vmem_limit_bytes=VMEM_LIMIT_BYTES),
        name="mix",
    )(o, u, u, u, gates, x1, dw_w, dw_b, cg, cb, wc, wa, wo, g2, b2)


def _encoder_layer(x, l, p):
    lambda_init = 0.8 - 0.6 * math.exp(-0.3 * l)
    bsz, seq, _ = x.shape
    row = lambda a: a[l].reshape(1, -1).astype(f32)
    wb = lambda a: a[l].astype(bf16)

    x1 = _ffn_ln(x.reshape(bsz * seq, D_MODEL), wb(p['ffn1_w_gate']), wb(p['ffn1_w_up']), wb(p['ffn1_w_down']),
                 row(p['ln1_g']), row(p['ln1_b'])).reshape(bsz, seq, D_MODEL)
    q, k, v, u, gates = _in_proj(x1, wb(p['w_in']), row(p['b_gate']), *_rope_tables(seq))
    o = _diff_attn(q, k, v, row(p['lambda_q1']), row(p['lambda_k1']), row(p['lambda_q2']), row(p['lambda_k2']),
                   row(p['subln_g']), lambda_init)
    x2 = _mix(o, u, gates, x1, p['conv_dw_w'][l].reshape(CONV_K, CONV_CH).astype(f32), row(p['conv_dw_b']),
              row(p['conv_ln_g']), row(p['conv_ln_b']), wb(p['w_conv_proj']), wb(p['w_att_proj']),
              wb(p['w_out']), row(p['ln2_g']), row(p['ln2_b']))
    x3 = _ffn_ln(x2.reshape(bsz * seq, D_MODEL), wb(p['ffn2_w_gate']), wb(p['ffn2_w_up']), wb(p['ffn2_w_down']),
                 row(p['ln3_g']), row(p['ln3_b']))
    return x3.reshape(bsz, seq, D_MODEL)


def kernel(x_prompt, x_sample, ffn1_w_gate, ffn1_w_up, ffn1_w_down, ln1_g, ln1_b, w_in, b_gate, lambda_q1, lambda_k1, lambda_q2, lambda_k2, subln_g, w_att_proj, conv_dw_w, conv_dw_b, conv_ln_g, conv_ln_b, w_conv_proj, w_out, ln2_g, ln2_b, ffn2_w_gate, ffn2_w_up, ffn2_w_down, ln3_g, ln3_b):
    p = dict(ffn1_w_gate=ffn1_w_gate, ffn1_w_up=ffn1_w_up, ffn1_w_down=ffn1_w_down,
             ln1_g=ln1_g, ln1_b=ln1_b, w_in=w_in, b_gate=b_gate,
             lambda_q1=lambda_q1, lambda_k1=lambda_k1, lambda_q2=lambda_q2, lambda_k2=lambda_k2,
             subln_g=subln_g, w_att_proj=w_att_proj, conv_dw_w=conv_dw_w, conv_dw_b=conv_dw_b,
             conv_ln_g=conv_ln_g, conv_ln_b=conv_ln_b, w_conv_proj=w_conv_proj, w_out=w_out,
             ln2_g=ln2_g, ln2_b=ln2_b, ffn2_w_gate=ffn2_w_gate, ffn2_w_up=ffn2_w_up,
             ffn2_w_down=ffn2_w_down, ln3_g=ln3_g, ln3_b=ln3_b)
    outs = []
    for x in (x_prompt, x_sample):
        for l in range(DEPTH):
            x = _encoder_layer(x, l, p)
        outs.append(x)
    return tuple(outs)
```

```python
import functools
import math

import jax
import jax.numpy as jnp
from jax import lax
from jax.experimental import pallas as pl
from jax.experimental.pallas import tpu as pltpu

D_MODEL = 1024
DEPTH = 1
N_HEADS = 4
HEAD_DIM = 64
V_DIM = 2 * HEAD_DIM
ATT_WIDTH = N_HEADS * V_DIM
ROT_DIM = HEAD_DIM // 4
ROPE_THETA = 500000.0
CONV_CH = 512
CONV_K = 31
D_FF = 2816
N_BRANCH = 2
LN_EPS = 1e-5
ALPHA = (2.0 * DEPTH) ** 0.25
Q_COLS = N_HEADS * 2 * HEAD_DIM
IN_COLS = 2 * Q_COLS + ATT_WIDTH + 2 * CONV_CH + N_BRANCH * D_MODEL
LOG2_E = math.log2(math.e)

LANES = 128
SUBLANES = 8
BF16_SUBLANES = 16
VMEM_LIMIT_BYTES = 56 * 1024 * 1024

FFN_ROWS = 512
PROJ_ROWS = 512
ATTN_Q_ROWS = 1024
ATTN_SUB_ROWS = 128
ATTN_SCORE_SLOTS = 4
ATTN_KEY_CHUNK = 512
MIX_ROWS = 512
MIX_SUB_ROWS = 256
CONV_HALO = BF16_SUBLANES
CONV_CHUNK = 64

f32 = jnp.float32
bf16 = jnp.bfloat16


def _resident(shape):
    return pl.BlockSpec(shape, lambda *_: (0,) * len(shape), pipeline_mode=pl.Buffered(1))


def _layer_norm(x, g, b):
    mu = jnp.mean(x, axis=-1, keepdims=True)
    xc = x - mu
    var = jnp.mean(xc * xc, axis=-1, keepdims=True)
    return xc * lax.rsqrt(var + LN_EPS) * g + b


def _silu(x):
    return x * jax.nn.sigmoid(x)


def _dot(a, b):
    return jnp.dot(a, b, preferred_element_type=f32)


def _ffn_ln_kernel(x_ref, wg_ref, wu_ref, wd_ref, g_ref, b_ref, o_ref):
    x = x_ref[...]
    xb = x.astype(bf16)
    gate = _dot(xb, wg_ref[...])
    up = _dot(xb, wu_ref[...])
    h = (_silu(gate) * up).astype(bf16)
    y = _dot(h, wd_ref[...])
    o_ref[...] = _layer_norm(ALPHA * x + 0.5 * y, g_ref[...], b_ref[...])


def _ffn_ln(x, wg, wu, wd, g, b):
    n = x.shape[0]
    return pl.pallas_call(
        _ffn_ln_kernel,
        grid=(n // FFN_ROWS,),
        in_specs=[
            pl.BlockSpec((FFN_ROWS, D_MODEL), lambda i: (i, 0)),
            _resident((D_MODEL, D_FF)),
            _resident((D_MODEL, D_FF)),
            _resident((D_FF, D_MODEL)),
            _resident((1, D_MODEL)),
            _resident((1, D_MODEL)),
        ],
        out_specs=pl.BlockSpec((FFN_ROWS, D_MODEL), lambda i: (i, 0)),
        out_shape=jax.ShapeDtypeStruct((n, D_MODEL), f32),
        compiler_params=pltpu.CompilerParams(
            dimension_semantics=("arbitrary",), vmem_limit_bytes=VMEM_LIMIT_BYTES),
        name="ffn_ln",
    )(x, wg, wu, wd, g, b)


def _rope(z, cos_t, sin_lo, sin_hi):
    half = ROT_DIM // 2
    right = pltpu.roll(z, LANES - half, axis=1)
    left = pltpu.roll(z, half, axis=1)
    return z * cos_t + right * sin_lo + left * sin_hi


def _in_proj_kernel(x_ref, w_ref, bg_ref, cos_ref, slo_ref, shi_ref,
                    q_ref, k_ref, v_ref, u_ref, gate_ref):
    xb = x_ref[0].astype(bf16)
    cos_t, sin_lo, sin_hi = cos_ref[...], slo_ref[...], shi_ref[...]
    c_k, c_v, c_conv, c_gate = Q_COLS, 2 * Q_COLS, 2 * Q_COLS + ATT_WIDTH, 2 * Q_COLS + ATT_WIDTH + 2 * CONV_CH
    q_scale = (HEAD_DIM ** -0.5) * LOG2_E
    zq = _dot(xb, w_ref[:, :c_k])
    zk = _dot(xb, w_ref[:, c_k:c_v])
    for s in range(Q_COLS // LANES):
        lo = s * LANES
        q_ref[0, :, lo:lo + LANES] = (_rope(zq[:, lo:lo + LANES], cos_t, sin_lo, sin_hi) * q_scale).astype(bf16)
        k_ref[0, :, lo:lo + LANES] = _rope(zk[:, lo:lo + LANES], cos_t, sin_lo, sin_hi).astype(bf16)
    v_ref[0] = _dot(xb, w_ref[:, c_v:c_conv]).astype(bf16)
    zc = _dot(xb, w_ref[:, c_conv:c_gate])
    u_ref[0] = (zc[:, :CONV_CH] * jax.nn.sigmoid(zc[:, CONV_CH:])).astype(bf16)
    for s in range(N_BRANCH):
        lo = s * D_MODEL
        zg = _dot(xb, w_ref[:, c_gate + lo:c_gate + lo + D_MODEL])
        gate_ref[0, :, lo:lo + D_MODEL] = jax.nn.sigmoid(zg + bg_ref[:, lo:lo + D_MODEL]).astype(bf16)


def _in_proj(x1, w_in, b_gate, cos_t, sin_lo, sin_hi):
    bsz, seq, _ = x1.shape
    rows = PROJ_ROWS
    tok = lambda width: pl.BlockSpec((1, rows, width), lambda b, i: (b, i, 0))
    tab = pl.BlockSpec((rows, LANES), lambda b, i: (i, 0))
    out = lambda width: jax.ShapeDtypeStruct((bsz, seq, width), bf16)
    return pl.pallas_call(
        _in_proj_kernel,
        grid=(bsz, seq // rows),
        in_specs=[tok(D_MODEL), _resident((D_MODEL, IN_COLS)), _resident((1, N_BRANCH * D_MODEL)),
                  tab, tab, tab],
        out_specs=[tok(Q_COLS), tok(Q_COLS), tok(ATT_WIDTH), tok(CONV_CH), tok(N_BRANCH * D_MODEL)],
        out_shape=[out(Q_COLS), out(Q_COLS), out(ATT_WIDTH), out(CONV_CH), out(N_BRANCH * D_MODEL)],
        compiler_params=pltpu.CompilerParams(
            dimension_semantics=("arbitrary", "arbitrary"), vmem_limit_bytes=VMEM_LIMIT_BYTES),
        name="in_proj",
    )(x1, w_in, b_gate, cos_t, sin_lo, sin_hi)


def _rope_tables(seq):
    half = ROT_DIM // 2
    inv = ROPE_THETA ** (-jnp.arange(0, ROT_DIM, 2, dtype=f32) / ROT_DIM)
    ang = jnp.arange(seq, dtype=f32)[:, None] * inv[None, :]
    cos, sin = jnp.cos(ang), jnp.sin(ang)
    zeros = jnp.zeros((seq, HEAD_DIM - ROT_DIM), f32)
    zh = jnp.zeros((seq, half), f32)
    cos_t = jnp.concatenate([cos, cos, zeros + 1.0], axis=1)
    sin_lo = jnp.concatenate([-sin, zh, zeros], axis=1)
    sin_hi = jnp.concatenate([zh, sin, zeros], axis=1)
    rep = LANES // HEAD_DIM
    return tuple(jnp.tile(t, (1, rep)) for t in (cos_t, sin_lo, sin_hi))


def _diff_attn_kernel(lambda_init, q_ref, k_ref, v_ref, lq1_ref, lk1_ref, lq2_ref, lk2_ref, sg_ref,
                      o_ref, vaug_ref, s_ref):
    seq = k_ref.shape[1]
    sub, kc = ATTN_SUB_ROWS, ATTN_KEY_CHUNK

    @pl.when(pl.program_id(2) == 0)
    def _():
        vaug_ref[:, :V_DIM] = v_ref[0]
        vaug_ref[:, V_DIM:] = jnp.ones((seq, V_DIM), bf16)

    lam = (jnp.exp(jnp.sum(lq1_ref[...] * lk1_ref[...], axis=-1, keepdims=True))
           - jnp.exp(jnp.sum(lq2_ref[...] * lk2_ref[...], axis=-1, keepdims=True)) + lambda_init)

    for t in range(q_ref.shape[1] // sub):
        q = q_ref[0, t * sub:(t + 1) * sub, :]
        lane = lax.broadcasted_iota(jnp.int32, q.shape, 1)
        zero = jnp.zeros_like(q)
        q2 = jnp.concatenate([jnp.where(lane < HEAD_DIM, q, zero), jnp.where(lane >= HEAD_DIM, q, zero)], axis=0)
        m_part = None
        for c in range(seq // kc):
            s = lax.dot_general(q2, k_ref[0, c * kc:(c + 1) * kc, :], (((1,), (1,)), ((), ())),
                                preferred_element_type=f32)
            s_ref[t % ATTN_SCORE_SLOTS, :, c * kc:(c + 1) * kc] = s
            for j in range(kc // LANES):
                blk = s[:, j * LANES:(j + 1) * LANES]
                m_part = blk if m_part is None else jnp.maximum(m_part, blk)
        m = jnp.max(m_part, axis=-1, keepdims=True)
        acc = jnp.zeros((2 * sub, 2 * V_DIM), f32)
        for c in range(seq // kc):
            e = jnp.exp2(s_ref[t % ATTN_SCORE_SLOTS, :, c * kc:(c + 1) * kc] - m).astype(bf16)
            acc = acc + _dot(e, vaug_ref[c * kc:(c + 1) * kc, :])
        o = acc[:, :V_DIM] / acc[:, V_DIM:]
        d = o[:sub] - lam * o[sub:]
        ms = jnp.mean(d * d, axis=-1, keepdims=True)
        o_ref[0, t * sub:(t + 1) * sub, :] = (
            d * lax.rsqrt(ms + LN_EPS) * sg_ref[...] * (1.0 - lambda_init)).astype(bf16)


def _diff_attn(q, k, v, lq1, lk1, lq2, lk2, subln_g, lambda_init):
    bsz, seq, _ = q.shape
    tq = ATTN_Q_ROWS
    kv = pl.BlockSpec((1, seq, V_DIM), lambda b, h, i: (b, 0, h))
    qo = pl.BlockSpec((1, tq, V_DIM), lambda b, h, i: (b, i, h))
    return pl.pallas_call(
        functools.partial(_diff_attn_kernel, lambda_init),
        grid=(bsz, N_HEADS, seq // tq),
        in_specs=[qo, kv, kv] + [_resident((1, HEAD_DIM))] * 4 + [_resident((1, V_DIM))],
        out_specs=qo,
        out_shape=jax.ShapeDtypeStruct((bsz, seq, ATT_WIDTH), bf16),
        scratch_shapes=[pltpu.VMEM((seq, 2 * V_DIM), bf16),
                        pltpu.VMEM((ATTN_SCORE_SLOTS, 2 * ATTN_SUB_ROWS, seq), f32)],
        compiler_params=pltpu.CompilerParams(
            dimension_semantics=("arbitrary",) * 3, vmem_limit_bytes=VMEM_LIMIT_BYTES),
        name="diff_attn",
    )(q, k, v, lq1, lk1, lq2, lk2, subln_g)


def _mix_kernel(o_ref, u_ref, up_ref, un_ref, gate_ref, x_ref,
                dw_ref, db_ref, cg_ref, cb_ref, wc_ref, wa_ref, wo_ref, g2_ref, b2_ref,
                y_ref, ush_ref):
    i = pl.program_id(1)
    rows = u_ref.shape[1]
    prev = up_ref[0].astype(f32)
    nxt = un_ref[0].astype(f32)
    ush_ref[0, 0:CONV_HALO, :] = jnp.where(i > 0, prev, jnp.zeros_like(prev))
    ush_ref[0, CONV_HALO:CONV_HALO + rows, :] = u_ref[0].astype(f32)
    ush_ref[0, CONV_HALO + rows:, :] = jnp.where(i < pl.num_programs(1) - 1, nxt, jnp.zeros_like(nxt))
    span = rows + 2 * CONV_HALO - SUBLANES
    for r in range(1, SUBLANES):
        ush_ref[r, 0:span, :] = ush_ref[0, r:r + span, :]

    first = CONV_HALO - CONV_K // 2
    for t in range(rows // MIX_SUB_ROWS):
        base = t * MIX_SUB_ROWS
        chunks = []
        for r0 in range(base, base + MIX_SUB_ROWS, CONV_CHUNK):
            acc = jnp.zeros((CONV_CHUNK, CONV_CH), f32)
            for j in range(CONV_K):
                shift, start = (first + j) % SUBLANES, r0 + (first + j) // SUBLANES * SUBLANES
                tap = jnp.concatenate([dw_ref[j]] * (CONV_CHUNK // SUBLANES), axis=0)
                acc = acc + tap * ush_ref[shift, start:start + CONV_CHUNK, :]
            chunks.append(acc)
        conv = jnp.concatenate(chunks, axis=0) + db_ref[...]
        c_in = _silu(_layer_norm(conv, cg_ref[...], cb_ref[...])).astype(bf16)
        c = _dot(c_in, wc_ref[...])
        a = _dot(o_ref[0, base:base + MIX_SUB_ROWS, :], wa_ref[...])
        gate = gate_ref[0, base:base + MIX_SUB_ROWS, :].astype(f32)
        m = (gate[:, :D_MODEL] * a + gate[:, D_MODEL:] * c).astype(bf16)
        y_ref[0, base:base + MIX_SUB_ROWS, :] = _layer_norm(
            ALPHA * x_ref[0, base:base + MIX_SUB_ROWS, :] + _dot(m, wo_ref[...]), g2_ref[...], b2_ref[...])


def _mix(o, u, gates, x1, dw_w, dw_b, cg, cb, wc, wa, wo, g2, b2):
    bsz, seq, _ = x1.shape
    rows = MIX_ROWS
    per_tile = rows // CONV_HALO
    last_halo = seq // CONV_HALO - 1
    tok = lambda width: pl.BlockSpec((1, rows, width), lambda b, i: (b, i, 0))
    halo_prev = pl.BlockSpec((1, CONV_HALO, CONV_CH), lambda b, i: (b, jnp.maximum(i * per_tile - 1, 0), 0))
    halo_next = pl.BlockSpec((1, CONV_HALO, CONV_CH), lambda b, i: (b, jnp.minimum((i + 1) * per_tile, last_halo), 0))
    return pl.pallas_call(
        _mix_kernel,
        grid=(bsz, seq // rows),
        in_specs=[tok(ATT_WIDTH), tok(CONV_CH), halo_prev, halo_next, tok(N_BRANCH * D_MODEL), tok(D_MODEL),
                  _resident((CONV_K, SUBLANES, CONV_CH)), _resident((1, CONV_CH)), _resident((1, CONV_CH)),
                  _resident((1, CONV_CH)), _resident((CONV_CH, D_MODEL)), _resident((ATT_WIDTH, D_MODEL)),
                  _resident((D_MODEL, D_MODEL)), _resident((1, D_MODEL)), _resident((1, D_MODEL))],
        out_specs=tok(D_MODEL),
        out_shape=jax.ShapeDtypeStruct((bsz, seq, D_MODEL), f32),
        scratch_shapes=[pltpu.VMEM((SUBLANES, rows + 2 * CONV_HALO, CONV_CH), f32)],
        compiler_params=pltpu.CompilerParams(
            dimension_semantics=("arbitrary", "arbitrary"), vmem_limit_bytes=VMEM_LIMIT_BYTES),
        name="mix",
    )(o, u, u, u, gates, x1, dw_w, dw_b, cg, cb, wc, wa, wo, g2, b2)


def _encoder_layer(x, l, p):
    lambda_init = 0.8 - 0.6 * math.exp(-0.3 * l)
    bsz, seq, _ = x.shape
    row = lambda a: a[l].reshape(1, -1).astype(f32)
    wb = lambda a: a[l].astype(bf16)
    dw_w = jnp.broadcast_to(p['conv_dw_w'][l].reshape(CONV_K, 1, CONV_CH).astype(f32), (CONV_K, SUBLANES, CONV_CH))

    x1 = _ffn_ln(x.reshape(bsz * seq, D_MODEL), wb(p['ffn1_w_gate']), wb(p['ffn1_w_up']), wb(p['ffn1_w_down']),
                 row(p['ln1_g']), row(p['ln1_b'])).reshape(bsz, seq, D_MODEL)
    q, k, v, u, gates = _in_proj(x1, wb(p['w_in']), row(p['b_gate']), *_rope_tables(seq))
    o = _diff_attn(q, k, v, row(p['lambda_q1']), row(p['lambda_k1']), row(p['lambda_q2']), row(p['lambda_k2']),
                   row(p['subln_g']), lambda_init)
    x2 = _mix(o, u, gates, x1, dw_w, row(p['conv_dw_b']),
              row(p['conv_ln_g']), row(p['conv_ln_b']), wb(p['w_conv_proj']), wb(p['w_att_proj']),
              wb(p['w_out']), row(p['ln2_g']), row(p['ln2_b']))
    x3 = _ffn_ln(x2.reshape(bsz * seq, D_MODEL), wb(p['ffn2_w_gate']), wb(p['ffn2_w_up']), wb(p['ffn2_w_down']),
                 row(p['ln3_g']), row(p['ln3_b']))
    return x3.reshape(bsz, seq, D_MODEL)


def kernel(x_prompt, x_sample, ffn1_w_gate, ffn1_w_up, ffn1_w_down, ln1_g, ln1_b, w_in, b_gate, lambda_q1, lambda_k1, lambda_q2, lambda_k2, subln_g, w_att_proj, conv_dw_w, conv_dw_b, conv_ln_g, conv_ln_b, w_conv_proj, w_out, ln2_g, ln2_b, ffn2_w_gate, ffn2_w_up, ffn2_w_down, ln3_g, ln3_b):
    p = dict(ffn1_w_gate=ffn1_w_gate, ffn1_w_up=ffn1_w_up, ffn1_w_down=ffn1_w_down,
             ln1_g=ln1_g, ln1_b=ln1_b, w_in=w_in, b_gate=b_gate,
             lambda_q1=lambda_q1, lambda_k1=lambda_k1, lambda_q2=lambda_q2, lambda_k2=lambda_k2,
             subln_g=subln_g, w_att_proj=w_att_proj, conv_dw_w=conv_dw_w, conv_dw_b=conv_dw_b,
             conv_ln_g=conv_ln_g, conv_ln_b=conv_ln_b, w_conv_proj=w_conv_proj, w_out=w_out,
             ln2_g=ln2_g, ln2_b=ln2_b, ffn2_w_gate=ffn2_w_gate, ffn2_w_up=ffn2_w_up,
             ffn2_w_down=ffn2_w_down, ln3_g=ln3_g, ln3_b=ln3_b)
    outs = []
    for x in (x_prompt, x_sample):
        for l in range(DEPTH):
            x = _encoder_layer(x, l, p)
        outs.append(x)
    return tuple(outs)
```
